```python
import jax, jax.numpy as jnp
from jax import lax
import numpy as np

D_MODEL = 1024
BATCH = 8
SEQ = 4096
DEPTH = 1

CONV_WIDTH = D_MODEL
CONV_KERNEL = 31
HEAD_DIM = 64
N_Q_HEADS = 16
N_KV_HEADS = 4
GROUP = N_Q_HEADS // N_KV_HEADS
ATTN_WIDTH = N_Q_HEADS * HEAD_DIM
KV_WIDTH = N_KV_HEADS * HEAD_DIM
WINDOW = 128
BLOCK = 128
ROPE_THETA = 10000.0
RMS_EPS = 1e-5
LN_EPS = 1e-5
N_BRANCHES = 2

SECTION_WIDTHS = (
    2 * CONV_WIDTH,
    CONV_WIDTH,
    ATTN_WIDTH,
    KV_WIDTH,
    KV_WIDTH,
    ATTN_WIDTH,
    N_BRANCHES * D_MODEL,
)
SPLIT_POINTS = tuple(int(v) for v in np.cumsum(SECTION_WIDTHS)[:-1])
IN_WIDTH = sum(SECTION_WIDTHS)

kernel_name = "hybrid_conformer_conv_swa_sink_gated_block"


def rmsnorm(x, g):
    xf = x.astype(jnp.float32)
    y = xf * lax.rsqrt(jnp.mean(xf * xf, axis=-1, keepdims=True) + RMS_EPS)
    return (y * g.astype(jnp.float32)).astype(x.dtype)


def layernorm(x, g, b):
    xf = x.astype(jnp.float32)
    mu = jnp.mean(xf, axis=-1, keepdims=True)
    var = jnp.mean(jnp.square(xf - mu), axis=-1, keepdims=True)
    y = (xf - mu) * lax.rsqrt(var + LN_EPS)
    return (y * g.astype(jnp.float32) + b.astype(jnp.float32)).astype(x.dtype)


def rope(t, pos):
    inv_freq = ROPE_THETA ** (-jnp.arange(0, HEAD_DIM, 2, dtype=jnp.float32) / HEAD_DIM)
    ang = pos.astype(jnp.float32)[:, None] * inv_freq[None, :]
    cos = jnp.cos(ang)[None, :, None, :]
    sin = jnp.sin(ang)[None, :, None, :]
    tf = t.astype(jnp.float32)
    t1, t2 = jnp.split(tf, 2, axis=-1)
    out = jnp.concatenate([t1 * cos - t2 * sin, t2 * cos + t1 * sin], axis=-1)
    return out.astype(t.dtype)


def conformer_conv(glu_in, w_dw, b_dw, ln_g, ln_b):
    a, b = jnp.split(glu_in, 2, axis=-1)
    h = a * jax.nn.sigmoid(b)
    h = lax.conv_general_dilated(
        h, w_dw[:, None, :].astype(h.dtype), window_strides=(1,),
        padding=[(CONV_KERNEL - 1, 0)],
        dimension_numbers=("NWC", "WIO", "NWC"),
        feature_group_count=CONV_WIDTH) + b_dw
    h = layernorm(h, ln_g, ln_b)
    return jax.nn.silu(h)


def sliding_window_attention(q, k, v, sinks):
    B, S = q.shape[0], q.shape[1]
    nb = S // BLOCK
    qb = q.reshape(B, nb, BLOCK, N_KV_HEADS, GROUP, HEAD_DIM)

    def band(t):
        tb = t.reshape(B, nb, BLOCK, N_KV_HEADS, HEAD_DIM)
        prev = jnp.pad(tb, ((0, 0), (1, 0), (0, 0), (0, 0), (0, 0)))[:, :-1]
        return jnp.concatenate([prev, tb], axis=2)

    kb, vb = band(k), band(v)
    scores = jnp.einsum("bnqhgd,bnkhd->bnhgqk", qb, kb).astype(jnp.float32) * (HEAD_DIM ** -0.5)
    qi = jnp.arange(BLOCK)[:, None]
    kj = jnp.arange(2 * BLOCK)[None, :]
    rel = qi + BLOCK - kj
    in_window = (rel >= 0) & (rel < WINDOW)
    key_pos = jnp.arange(nb)[:, None, None] * BLOCK - BLOCK + kj[None]
    mask = in_window[None] & (key_pos >= 0)
    scores = jnp.where(mask[None, :, None, None], scores, jnp.float32(-1e30))
    sink = sinks.astype(jnp.float32).reshape(N_KV_HEADS, GROUP)[None, None, :, :, None, None]
    m = jnp.maximum(jnp.max(scores, axis=-1, keepdims=True), sink)
    p = jnp.exp(scores - m)
    denom = jnp.sum(p, axis=-1, keepdims=True) + jnp.exp(sink - m)
    probs = (p / denom).astype(v.dtype)
    out = jnp.einsum("bnhgqk,bnkhd->bnqhgd", probs, vb)
    return out.reshape(B, S, ATTN_WIDTH)


def setup_inputs(seed: int = 0) -> dict:
    key = jax.random.key(seed)
    ks = jax.random.split(key, 13)
    f32 = jnp.float32
    x = jax.random.normal(ks[0], (BATCH, SEQ, D_MODEL), f32)
    norm_g = 1.0 + 0.02 * jax.random.normal(ks[1], (DEPTH, D_MODEL), f32)
    w_in = jax.random.normal(ks[2], (DEPTH, D_MODEL, IN_WIDTH), f32) * D_MODEL ** -0.5
    conv_dw_w = jax.random.normal(ks[3], (DEPTH, CONV_KERNEL, CONV_WIDTH), f32) * CONV_KERNEL ** -0.5
    conv_dw_b = 0.02 * jax.random.normal(ks[4], (DEPTH, CONV_WIDTH), f32)
    conv_ln_g = 1.0 + 0.02 * jax.random.normal(ks[5], (DEPTH, CONV_WIDTH), f32)
    conv_ln_b = 0.02 * jax.random.normal(ks[6], (DEPTH, CONV_WIDTH), f32)
    w_conv_out = jax.random.normal(ks[7], (DEPTH, CONV_WIDTH, D_MODEL), f32) * CONV_WIDTH ** -0.5
    attn_sinks = 0.5 * jax.random.normal(ks[8], (DEPTH, N_Q_HEADS), f32)
    w_attn_out = jax.random.normal(ks[9], (DEPTH, ATTN_WIDTH, D_MODEL), f32) * ATTN_WIDTH ** -0.5
    w_out = jax.random.normal(ks[10], (DEPTH, D_MODEL, D_MODEL), f32) * D_MODEL ** -0.5
    final_norm_g = 1.0 + 0.02 * jax.random.normal(ks[11], (D_MODEL,), f32)
    return {"x": x, "norm_g": norm_g, "w_in": w_in, "conv_dw_w": conv_dw_w,
            "conv_dw_b": conv_dw_b, "conv_ln_g": conv_ln_g, "conv_ln_b": conv_ln_b,
            "w_conv_out": w_conv_out, "attn_sinks": attn_sinks, "w_attn_out": w_attn_out,
            "w_out": w_out, "final_norm_g": final_norm_g}


def reference(x, norm_g, w_in, conv_dw_w, conv_dw_b, conv_ln_g, conv_ln_b,
              w_conv_out, attn_sinks, w_attn_out, w_out, final_norm_g):
    B, S = x.shape[0], x.shape[1]
    pos = jnp.arange(S, dtype=jnp.int32)
    for l in range(DEPTH):
        h = rmsnorm(x, norm_g[l])
        proj = jnp.einsum("bsd,de->bse", h, w_in[l])
        glu_in, conv_gate, q, k, v, attn_gate, merge_logits = jnp.split(proj, SPLIT_POINTS, axis=-1)

        c = conformer_conv(glu_in, conv_dw_w[l], conv_dw_b[l], conv_ln_g[l], conv_ln_b[l])
        y_conv = jnp.einsum("bsc,cd->bsd", c * jax.nn.silu(conv_gate), w_conv_out[l])

        q = rope(q.reshape(B, S, N_Q_HEADS, HEAD_DIM), pos)
        k = rope(k.reshape(B, S, N_KV_HEADS, HEAD_DIM), pos)
        v = v.reshape(B, S, N_KV_HEADS, HEAD_DIM)
        a = sliding_window_attention(q, k, v, attn_sinks[l])
        y_attn = jnp.einsum("bsa,ad->bsd", a * jax.nn.silu(attn_gate), w_attn_out[l])

        gates = jax.nn.sigmoid(merge_logits)
        g_conv, g_attn = jnp.split(gates, 2, axis=-1)
        merged = g_conv * y_conv + g_attn * y_attn
        x = x + jnp.einsum("bsd,de->bse", merged, w_out[l])
    return rmsnorm(x, final_norm_g)
```

```python
import functools

import numpy as np
import jax
import jax.numpy as jnp
from jax import lax
from jax.experimental import pallas as pl
from jax.experimental.pallas import tpu as pltpu

F32 = jnp.float32
BF16 = jnp.bfloat16

D_MODEL = 1024
CONV_KERNEL = 31
HEAD_DIM = 64
HALF_DIM = HEAD_DIM // 2
N_Q_HEADS = 16
N_KV_HEADS = 4
GROUP = N_Q_HEADS // N_KV_HEADS
ATTN_WIDTH = N_Q_HEADS * HEAD_DIM
KV_WIDTH = N_KV_HEADS * HEAD_DIM
WINDOW = 128
ROPE_THETA = 10000.0
RMS_EPS = 1e-5
LN_EPS = 1e-5
MASK_VALUE = -1e30

OFF_GLU_A = 0
OFF_GLU_B = OFF_GLU_A + D_MODEL
OFF_CONV_GATE = OFF_GLU_B + D_MODEL
OFF_Q = OFF_CONV_GATE + D_MODEL
OFF_K = OFF_Q + ATTN_WIDTH
OFF_V = OFF_K + KV_WIDTH
OFF_ATTN_GATE = OFF_V + KV_WIDTH
OFF_MERGE = OFF_ATTN_GATE + ATTN_WIDTH
IN_WIDTH = OFF_MERGE + 2 * D_MODEL

SUBLANES = 8
LANES = 128

TM = 256
TQ = 256
CONV_HALO = 32
CONV_ROWS = 64
VMEM_LIMIT = 56 * 1024 * 1024


def _sigmoid(v):
    return 1.0 / (1.0 + jnp.exp(-v))


def _silu(v):
    return v * _sigmoid(v)


def _dot(a, b):
    return jnp.dot(a, b, preferred_element_type=F32)


def _proj_body(x_ref, ng_ref, w_ref, wqvt_ref, cw_ref, cb_ref, lg_ref, lb_ref,
               cosq_ref, sinq_ref, cosk_ref, sink_ref,
               cg_ref, qt_ref, k_ref, vt_ref, sga_ref, gates_ref,
               gbuf, shbuf, cbuf):
    s = pl.program_id(1)
    x = x_ref[0]
    ms = jnp.mean(x * x, axis=-1, keepdims=True)
    hf = x * lax.rsqrt(ms + RMS_EPS) * ng_ref[...]
    h = hf.astype(BF16)
    ht = hf.T.astype(BF16)

    def proj(lo, hi):
        return _dot(h, w_ref[:, lo:hi])

    glu = proj(OFF_GLU_A, OFF_GLU_B) * _sigmoid(proj(OFF_GLU_B, OFF_CONV_GATE))

    @pl.when(s == 0)
    def _():
        gbuf[0:CONV_HALO, :] = jnp.zeros((CONV_HALO, D_MODEL), F32)

    gbuf[CONV_HALO:CONV_HALO + TM, :] = glu

    first_row = CONV_HALO - (CONV_KERNEL - 1)
    shift_rows = TM + CONV_HALO - SUBLANES
    for c in range(D_MODEL // LANES):
        cols = slice(c * LANES, (c + 1) * LANES)
        for r in range(1, SUBLANES):
            shbuf[r, 0:shift_rows, :] = gbuf[r:r + shift_rows, cols]
        taps = [cw_ref[j:j + 1, cols] for j in range(CONV_KERNEL)]
        bias = cb_ref[:, cols]

        def chunk(rc, carry):
            r0 = pl.multiple_of(rc * CONV_ROWS, CONV_ROWS)
            acc = jnp.zeros((CONV_ROWS, LANES), F32)
            for j in range(CONV_KERNEL):
                off = first_row + j
                r, a = off % SUBLANES, off // SUBLANES
                start = pl.multiple_of(r0 + a * SUBLANES, SUBLANES)
                if r == 0:
                    win = gbuf[pl.ds(start, CONV_ROWS), cols]
                else:
                    win = shbuf[r, pl.ds(start, CONV_ROWS), :]
                acc = acc + win * taps[j]
            cbuf[pl.ds(r0, CONV_ROWS), cols] = acc + bias
            return carry

        lax.fori_loop(0, TM // CONV_ROWS, chunk, 0)

    gbuf[0:CONV_HALO, :] = gbuf[TM:TM + CONV_HALO, :]

    conv = cbuf[...]
    mu = jnp.mean(conv, axis=-1, keepdims=True)
    cen = conv - mu
    var = jnp.mean(cen * cen, axis=-1, keepdims=True)
    ln = cen * lax.rsqrt(var + LN_EPS) * lg_ref[...] + lb_ref[...]
    cg = _silu(ln) * _silu(proj(OFF_CONV_GATE, OFF_Q))
    cg_ref[0] = cg.astype(BF16)

    sga_ref[0] = _silu(proj(OFF_ATTN_GATE, OFF_MERGE)).astype(BF16)
    gates_ref[0] = _sigmoid(proj(OFF_MERGE, IN_WIDTH)).astype(BF16)

    kf = proj(OFF_K, OFF_V)
    lane = lax.broadcasted_iota(jnp.int32, (TM, LANES), 1)
    upper_half = (lane % HEAD_DIM) >= HALF_DIM
    cosk = cosk_ref[...]
    sink = sink_ref[...]
    kparts = []
    for c in range(KV_WIDTH // LANES):
        xs = kf[:, c * LANES:(c + 1) * LANES]
        rot = jnp.where(upper_half,
                        pltpu.roll(xs, HALF_DIM, 1),
                        pltpu.roll(xs, LANES - HALF_DIM, 1))
        kparts.append(xs * cosk + rot * sink)
    k_ref[0] = jnp.concatenate(kparts, axis=1).astype(BF16)

    qvt = _dot(wqvt_ref[...], ht)
    cosq = cosq_ref[...]
    sinq = sinq_ref[...]
    for hd in range(N_Q_HEADS):
        lo = hd * HEAD_DIM
        t1 = qvt[lo:lo + HALF_DIM]
        t2 = qvt[lo + HALF_DIM:lo + HEAD_DIM]
        qt_ref[0, lo:lo + HALF_DIM, :] = (t1 * cosq - t2 * sinq).astype(BF16)
        qt_ref[0, lo + HALF_DIM:lo + HEAD_DIM, :] = (t2 * cosq + t1 * sinq).astype(BF16)
    vt_ref[0] = qvt[ATTN_WIDTH:ATTN_WIDTH + KV_WIDTH].astype(BF16)


def _mix_body(qt_ref, k_ref, kp_ref, vt_ref, vtp_ref, sga_ref, gates_ref, cg_ref, x_ref,
              wc_ref, wa_ref, wo_ref, bias_ref, sinkrow_ref, fg_ref,
              o_ref, at_scr):
    s = pl.program_id(1)
    first_variant = jnp.where(s == 0, 0, 1)
    gq = GROUP * WINDOW
    for qb in range(TQ // WINDOW):
        q_lanes = slice(qb * WINDOW, (qb + 1) * WINDOW)
        if qb == 0:
            kband = jnp.concatenate([kp_ref[0], k_ref[0, 0:WINDOW, :]], axis=0)
            vband = jnp.concatenate([vtp_ref[0], vt_ref[0, :, 0:WINDOW]], axis=1)
            bias = bias_ref[first_variant]
        else:
            kband = k_ref[0, (qb - 1) * WINDOW:(qb + 1) * WINDOW, :]
            vband = vt_ref[0, :, (qb - 1) * WINDOW:(qb + 1) * WINDOW]
            bias = bias_ref[1]
        for hk in range(N_KV_HEADS):
            rq = jnp.concatenate(
                [qt_ref[0, (hk * GROUP + g) * HEAD_DIM:(hk * GROUP + g + 1) * HEAD_DIM, q_lanes]
                 for g in range(GROUP)], axis=1)
            blocks = []
            if hk > 0:
                blocks.append(jnp.zeros((hk * HEAD_DIM, gq), BF16))
            blocks.append(rq)
            if hk < N_KV_HEADS - 1:
                blocks.append(jnp.zeros(((N_KV_HEADS - 1 - hk) * HEAD_DIM, gq), BF16))
            rhs = jnp.concatenate(blocks, axis=0)
            st = _dot(kband, rhs) + bias
            sink = sinkrow_ref[hk]
            m = jnp.maximum(jnp.max(st, axis=0, keepdims=True), sink)
            p = jnp.exp(st - m)
            den = jnp.sum(p, axis=0, keepdims=True) + jnp.exp(sink - m)
            ot = _dot(vband[hk * HEAD_DIM:(hk + 1) * HEAD_DIM, :], p.astype(BF16))
            ot = ot * (1.0 / den)
            for g in range(GROUP):
                lo = (hk * GROUP + g) * HEAD_DIM
                at_scr[lo:lo + HEAD_DIM, q_lanes] = ot[:, g * WINDOW:(g + 1) * WINDOW]

    attn = at_scr[...].T
    y_attn = _dot((attn * sga_ref[0].astype(F32)).astype(BF16), wa_ref[...])
    y_conv = _dot(cg_ref[0], wc_ref[...])
    gates = gates_ref[0].astype(F32)
    merged = gates[:, :D_MODEL] * y_conv + gates[:, D_MODEL:] * y_attn
    xo = x_ref[0] + _dot(merged.astype(BF16), wo_ref[...])
    ms = jnp.mean(xo * xo, axis=-1, keepdims=True)
    o_ref[0] = xo * lax.rsqrt(ms + RMS_EPS) * fg_ref[...]


def _resident(shape):
    zeros = (0,) * len(shape)
    return pl.BlockSpec(shape, lambda b, s: zeros, pipeline_mode=pl.Buffered(1))


def _rope_tables(seq):
    inv_freq = ROPE_THETA ** (-jnp.arange(0, HEAD_DIM, 2, dtype=F32) / HEAD_DIM)
    ang = jnp.arange(seq, dtype=F32)[:, None] * inv_freq[None, :]
    cos, sin = jnp.cos(ang), jnp.sin(ang)
    scale = HEAD_DIM ** -0.5
    cosq = (scale * cos).T
    sinq = (scale * sin).T
    reps = LANES // HALF_DIM
    cosk = jnp.tile(cos, (1, reps))
    sign = jnp.where((jnp.arange(LANES) % HEAD_DIM) >= HALF_DIM, 1.0, -1.0).astype(F32)
    sink = jnp.tile(sin, (1, reps)) * sign[None, :]
    return cosq, sinq, cosk, sink


def _band_bias():
    kj = np.arange(2 * WINDOW)[:, None]
    qi = np.arange(WINDOW)[None, :]
    rel = qi + WINDOW - kj
    in_window = (rel >= 0) & (rel < WINDOW)
    later = np.where(in_window, 0.0, MASK_VALUE)
    first = np.where(in_window & (kj >= WINDOW), 0.0, MASK_VALUE)
    both = np.stack([first, later]).astype(np.float32)
    return jnp.asarray(np.tile(both, (1, 1, GROUP)))


def kernel(x, norm_g, w_in, conv_dw_w, conv_dw_b, conv_ln_g, conv_ln_b, w_conv_out,
           attn_sinks, w_attn_out, w_out, final_norm_g):
    batch, seq, d = x.shape
    assert d == D_MODEL and seq % TM == 0 and seq % TQ == 0
    assert norm_g.shape[0] == 1, "single-layer block"

    w = w_in[0].astype(BF16)
    wqvt = jnp.concatenate([w_in[0][:, OFF_Q:OFF_K], w_in[0][:, OFF_V:OFF_ATTN_GATE]],
                           axis=1).T.astype(BF16)
    cosq, sinq, cosk, sink = _rope_tables(seq)
    row = lambda v: v.reshape(1, -1).astype(F32)

    tok = lambda width: pl.BlockSpec((1, TM, width), lambda b, s: (b, s, 0))
    feat = lambda rows: pl.BlockSpec((1, rows, TM), lambda b, s: (b, 0, s))
    cg, qt, k, vt, sga, gates = pl.pallas_call(
        _proj_body,
        grid=(batch, seq // TM),
        in_specs=[
            tok(D_MODEL),
            _resident((1, D_MODEL)),
            _resident((D_MODEL, IN_WIDTH)),
            _resident((ATTN_WIDTH + KV_WIDTH, D_MODEL)),
            _resident((CONV_KERNEL, D_MODEL)),
            _resident((1, D_MODEL)),
            _resident((1, D_MODEL)),
            _resident((1, D_MODEL)),
            pl.BlockSpec((HALF_DIM, TM), lambda b, s: (0, s)),
            pl.BlockSpec((HALF_DIM, TM), lambda b, s: (0, s)),
            pl.BlockSpec((TM, LANES), lambda b, s: (s, 0)),
            pl.BlockSpec((TM, LANES), lambda b, s: (s, 0)),
        ],
        out_specs=[tok(D_MODEL), feat(ATTN_WIDTH), tok(KV_WIDTH), feat(KV_WIDTH),
                   tok(ATTN_WIDTH), tok(2 * D_MODEL)],
        out_shape=[
            jax.ShapeDtypeStruct((batch, seq, D_MODEL), BF16),
            jax.ShapeDtypeStruct((batch, ATTN_WIDTH, seq), BF16),
            jax.ShapeDtypeStruct((batch, seq, KV_WIDTH), BF16),
            jax.ShapeDtypeStruct((batch, KV_WIDTH, seq), BF16),
            jax.ShapeDtypeStruct((batch, seq, ATTN_WIDTH), BF16),
            jax.ShapeDtypeStruct((batch, seq, 2 * D_MODEL), BF16),
        ],
        scratch_shapes=[
            pltpu.VMEM((TM + CONV_HALO, D_MODEL), F32),
            pltpu.VMEM((SUBLANES, TM + CONV_HALO, LANES), F32),
            pltpu.VMEM((TM, D_MODEL), F32),
        ],
        compiler_params=pltpu.CompilerParams(
            dimension_semantics=("arbitrary", "arbitrary"),
            vmem_limit_bytes=VMEM_LIMIT),
        name="proj_call",
    )(x, row(norm_g[0]), w, wqvt, conv_dw_w[0].astype(F32), row(conv_dw_b[0]),
      row(conv_ln_g[0]), row(conv_ln_b[0]), cosq, sinq, cosk, sink)

    sinkrow = jnp.repeat(attn_sinks[0].astype(F32).reshape(N_KV_HEADS, 1, GROUP),
                         WINDOW, axis=2)
    blocks_per_step = TQ // WINDOW
    prev = lambda s: jnp.maximum(s * blocks_per_step - 1, 0)
    tokq = lambda width: pl.BlockSpec((1, TQ, width), lambda b, s: (b, s, 0))
    out = pl.pallas_call(
        _mix_body,
        grid=(batch, seq // TQ),
        in_specs=[
            pl.BlockSpec((1, ATTN_WIDTH, TQ), lambda b, s: (b, 0, s)),
            tokq(KV_WIDTH),
            pl.BlockSpec((1, WINDOW, KV_WIDTH), lambda b, s: (b, prev(s), 0)),
            pl.BlockSpec((1, KV_WIDTH, TQ), lambda b, s: (b, 0, s)),
            pl.BlockSpec((1, KV_WIDTH, WINDOW), lambda b, s: (b, 0, prev(s))),
            tokq(ATTN_WIDTH),
            tokq(2 * D_MODEL),
            tokq(D_MODEL),
            tokq(D_MODEL),
            _resident((D_MODEL, D_MODEL)),
            _resident((ATTN_WIDTH, D_MODEL)),
            _resident((D_MODEL, D_MODEL)),
            _resident((2, 2 * WINDOW, GROUP * WINDOW)),
            _resident((N_KV_HEADS, 1, GROUP * WINDOW)),
            _resident((1, D_MODEL)),
        ],
        out_specs=tokq(D_MODEL),
        out_shape=jax.ShapeDtypeStruct((batch, seq, D_MODEL), x.dtype),
        scratch_shapes=[pltpu.VMEM((ATTN_WIDTH, TQ), F32)],
        compiler_params=pltpu.CompilerParams(
            dimension_semantics=("arbitrary", "arbitrary"),
            vmem_limit_bytes=VMEM_LIMIT),
        name="mix_call",
    )(qt, k, k, vt, vt, sga, gates, cg, x,
      w_conv_out[0].astype(BF16), w_attn_out[0].astype(BF16), w_out[0].astype(BF16),
      _band_bias(), sinkrow, row(final_norm_g))
    return out
```

```python
import numpy as np
import jax
import jax.numpy as jnp
from jax import lax
from jax.experimental import pallas as pl
from jax.experimental.pallas import tpu as pltpu

F32 = jnp.float32
BF16 = jnp.bfloat16

D_MODEL = 1024
CONV_KERNEL = 31
HEAD_DIM = 64
HALF_DIM = HEAD_DIM // 2
N_Q_HEADS = 16
N_KV_HEADS = 4
GROUP = N_Q_HEADS // N_KV_HEADS
ATTN_WIDTH = N_Q_HEADS * HEAD_DIM
KV_WIDTH = N_KV_HEADS * HEAD_DIM
WINDOW = 128
ROPE_THETA = 10000.0
RMS_EPS = 1e-5
LN_EPS = 1e-5
MASK_VALUE = -1e30

OFF_GLU_A = 0
OFF_GLU_B = OFF_GLU_A + D_MODEL
OFF_CONV_GATE = OFF_GLU_B + D_MODEL
OFF_Q = OFF_CONV_GATE + D_MODEL
OFF_K = OFF_Q + ATTN_WIDTH
OFF_V = OFF_K + KV_WIDTH
OFF_ATTN_GATE = OFF_V + KV_WIDTH
OFF_MERGE = OFF_ATTN_GATE + ATTN_WIDTH
IN_WIDTH = OFF_MERGE + 2 * D_MODEL

ROW_Q = 0
ROW_V = ROW_Q + ATTN_WIDTH
ROW_GATE = ROW_V + KV_WIDTH
FEAT_ROWS = ROW_GATE + ATTN_WIDTH

SUBLANES = 8
LANES = 128
N_STRIPS = D_MODEL // LANES

TM = 256
TQ = 512
CONV_HALO = 32
CONV_ROWS = 64
GLU_COLS = 256
PITCH_PAD = 512
VMEM_LIMIT = 56 * 1024 * 1024


def _dot(a, b):
    return jnp.dot(a, b, preferred_element_type=F32)


def _half_silu(half):
    return half * (1.0 + jnp.tanh(half))


def _proj_body(x_ref, ng_ref, w_ref, wft_ref, cw_ref, cb_ref, lg_ref, lb_ref,
               cosq_ref, sinq_ref, cosk_ref, sink_ref,
               cg_ref, qt_ref, k_ref, vt_ref, sga_ref, gates_ref,
               gbuf, cbuf, h_scr, ht_scr):
    s = pl.program_id(1)
    x = x_ref[0]
    ms = jnp.mean(x * x, axis=-1, keepdims=True)
    h_scr[...] = (x * lax.rsqrt(ms + RMS_EPS) * ng_ref[...]).astype(BF16)

    def transpose_h():
        ht_scr[...] = h_scr[...].T

    def proj(lo, hi):
        return _dot(h_scr[...], w_ref[:, lo:hi])

    def proj_t(lo, hi):
        return _dot(wft_ref[lo:hi, 0:D_MODEL], ht_scr[...])

    def glu_piece(g):
        def piece():
            a = OFF_GLU_A + g * GLU_COLS
            b = OFF_GLU_B + g * GLU_COLS
            glu = proj(a, a + GLU_COLS) * (1.0 + jnp.tanh(proj(b, b + GLU_COLS)))
            for i in range(GLU_COLS // LANES):
                c = g * (GLU_COLS // LANES) + i
                gbuf[0, c, CONV_HALO:CONV_HALO + TM, :] = glu[:, i * LANES:(i + 1) * LANES]
                for r in range(1, SUBLANES):
                    gbuf[r, c, 0:shifted_rows, :] = gbuf[0, c, r:r + shifted_rows, :]
        return piece

    @pl.when(s == 0)
    def _():
        gbuf[0, :, 0:CONV_HALO, :] = jnp.zeros((N_STRIPS, CONV_HALO, LANES), F32)

    shifted_rows = TM + CONV_HALO - SUBLANES
    first_row = CONV_HALO - (CONV_KERNEL - 1)

    def conv_strip(c):
        cols = slice(c * LANES, (c + 1) * LANES)
        taps = [cw_ref[j:j + 1, cols] for j in range(CONV_KERNEL)]
        bias = cb_ref[:, cols]
        for r0 in range(0, TM, CONV_ROWS):
            acc = bias
            for r in range(SUBLANES):
                js = [j for j in range(CONV_KERNEL) if (first_row + j) % SUBLANES == r]
                tops = [(first_row + j) // SUBLANES * SUBLANES + r0 for j in js]
                slab = gbuf[r, c, min(tops):max(tops) + CONV_ROWS, :]
                for j, top in zip(js, tops):
                    lo = top - min(tops)
                    acc = acc + slab[lo:lo + CONV_ROWS] * taps[j]
            cbuf[r0:r0 + CONV_ROWS, cols] = acc
        gbuf[0, c, 0:CONV_HALO, :] = gbuf[0, c, TM:TM + CONV_HALO, :]

    def conv_gate(lo, hi):
        cg_ref[0, :, lo:hi] = _half_silu(
            proj(OFF_CONV_GATE + lo, OFF_CONV_GATE + hi)).astype(BF16)

    def merge_gates(lo, hi):
        gates_ref[0, :, lo:hi] = (
            1.0 + jnp.tanh(proj(OFF_MERGE + lo, OFF_MERGE + hi))).astype(BF16)

    def keys():
        kf = proj(OFF_K, OFF_V)
        lane = lax.broadcasted_iota(jnp.int32, (TM, LANES), 1)
        upper_half = (lane % HEAD_DIM) >= HALF_DIM
        cosk = cosk_ref[...]
        sink = sink_ref[...]
        for c in range(KV_WIDTH // LANES):
            xs = kf[:, c * LANES:(c + 1) * LANES]
            rot = jnp.where(upper_half,
                            pltpu.roll(xs, HALF_DIM, 1),
                            pltpu.roll(xs, LANES - HALF_DIM, 1))
            k_ref[0, :, c * LANES:(c + 1) * LANES] = (xs * cosk + rot * sink).astype(BF16)

    def values():
        vt_ref[0] = proj_t(ROW_V, ROW_GATE).astype(BF16)

    def queries(lo, hi):
        qt = proj_t(ROW_Q + lo, ROW_Q + hi)
        cosq = cosq_ref[...]
        sinq = sinq_ref[...]
        for a in range(0, hi - lo, HEAD_DIM):
            t1 = qt[a:a + HALF_DIM]
            t2 = qt[a + HALF_DIM:a + HEAD_DIM]
            qt_ref[0, lo + a:lo + a + HALF_DIM, :] = (t1 * cosq - t2 * sinq).astype(BF16)
            qt_ref[0, lo + a + HALF_DIM:lo + a + HEAD_DIM, :] = (
                t2 * cosq + t1 * sinq).astype(BF16)

    def attn_gate(lo, hi):
        sga_ref[0, lo:hi, :] = _half_silu(proj_t(ROW_GATE + lo, ROW_GATE + hi)).astype(BF16)

    def layer_norm():
        for r0 in range(0, TM, CONV_ROWS):
            rows = slice(r0, r0 + CONV_ROWS)
            conv = cbuf[rows, :]
            mu = jnp.mean(conv, axis=-1, keepdims=True)
            cen = conv - mu
            var = jnp.mean(cen * cen, axis=-1, keepdims=True)
            half_ln = cen * lax.rsqrt(var + LN_EPS) * lg_ref[...] + lb_ref[...]
            gate = cg_ref[0, rows, :].astype(F32)
            cg_ref[0, rows, :] = (_half_silu(half_ln) * gate).astype(BF16)

    n_blocks = [0]

    def block(*steps):
        n_blocks[0] += 1

        @pl.when(s > -n_blocks[0])
        def _():
            for step in steps:
                step()

    half_q = ATTN_WIDTH // 2
    for g in range(D_MODEL // GLU_COLS):
        glu_piece(g)()
    block(lambda: conv_strip(0), lambda: conv_gate(0, 768),
          lambda: conv_strip(1), lambda: conv_gate(768, D_MODEL), lambda: merge_gates(0, 512),
          transpose_h)
    block(lambda: conv_strip(2), lambda: merge_gates(512, 1280),
          lambda: conv_strip(3), lambda: merge_gates(1280, 2 * D_MODEL))
    block(lambda: conv_strip(4), keys, lambda: queries(0, half_q),
          lambda: conv_strip(5), lambda: queries(half_q, ATTN_WIDTH), values)
    block(lambda: conv_strip(6), lambda: attn_gate(0, half_q),
          lambda: conv_strip(7), lambda: attn_gate(half_q, ATTN_WIDTH))
    block(layer_norm)


def _mix_body(qt_ref, k_ref, kp_ref, vt_ref, vtp_ref, sga_ref, gates_ref, cg_ref, x_ref,
              wout_ref, bias_ref, sinkrow_ref, fg_ref,
              o_ref, at_scr):
    s = pl.program_id(1)
    first_variant = jnp.where(s == 0, 0, 1)
    gq = GROUP * WINDOW

    def scores(qb, hk):
        q_lanes = slice(qb * WINDOW, (qb + 1) * WINDOW)
        if qb == 0:
            kband = jnp.concatenate([kp_ref[0], k_ref[0, 0:WINDOW, :]], axis=0)
            bias = bias_ref[first_variant]
        else:
            kband = k_ref[0, (qb - 1) * WINDOW:(qb + 1) * WINDOW, :]
            bias = bias_ref[1]
        rq = jnp.concatenate(
            [qt_ref[0, (hk * GROUP + g) * HEAD_DIM:(hk * GROUP + g + 1) * HEAD_DIM, q_lanes]
             for g in range(GROUP)], axis=1)
        blocks = []
        if hk > 0:
            blocks.append(jnp.zeros((hk * HEAD_DIM, gq), BF16))
        blocks.append(rq)
        if hk < N_KV_HEADS - 1:
            blocks.append(jnp.zeros(((N_KV_HEADS - 1 - hk) * HEAD_DIM, gq), BF16))
        rhs = jnp.concatenate(blocks, axis=0)
        return _dot(kband, rhs) + bias

    def finish(qb, hk, st):
        q_lanes = slice(qb * WINDOW, (qb + 1) * WINDOW)
        if qb == 0:
            vband = jnp.concatenate([vtp_ref[0], vt_ref[0, :, 0:WINDOW]], axis=1)
        else:
            vband = vt_ref[0, :, (qb - 1) * WINDOW:(qb + 1) * WINDOW]
        sink = sinkrow_ref[hk]
        m = jnp.maximum(jnp.max(st, axis=0, keepdims=True), sink)
        p = jnp.exp(st - m)
        den = jnp.sum(p, axis=0, keepdims=True) + jnp.exp(sink - m)
        ot = _dot(vband[hk * HEAD_DIM:(hk + 1) * HEAD_DIM, :], p.astype(BF16))
        ot = ot * (1.0 / den)
        for g in range(GROUP):
            lo = (hk * GROUP + g) * HEAD_DIM
            at_scr[lo:lo + HEAD_DIM, q_lanes] = ot[:, g * WINDOW:(g + 1) * WINDOW]

    def dense_pieces(qb):
        rows = slice(qb * WINDOW, (qb + 1) * WINDOW)
        vals = {}

        def conv_proj():
            vals["y_conv"] = _dot(cg_ref[0, rows, :], wout_ref[:, 0:D_MODEL])

        def attn_proj():
            gated = (at_scr[:, rows] * sga_ref[0, :, rows].astype(F32)).astype(BF16)
            vals["y_attn"] = lax.dot_general(
                gated, wout_ref[:, D_MODEL:2 * D_MODEL], (((0,), (0,)), ((), ())),
                preferred_element_type=F32)

        def out_proj():
            gates = gates_ref[0, rows, :].astype(F32)
            merged = (gates[:, :D_MODEL] * vals.pop("y_conv")
                      + gates[:, D_MODEL:] * vals.pop("y_attn"))
            vals["xo"] = x_ref[0, rows, :] + _dot(merged.astype(BF16),
                                                  wout_ref[:, 2 * D_MODEL:3 * D_MODEL])

        def final_norm():
            xo = vals.pop("xo")
            ms = jnp.mean(xo * xo, axis=-1, keepdims=True)
            o_ref[0, rows, :] = xo * lax.rsqrt(ms + RMS_EPS) * fg_ref[...]

        return [conv_proj, attn_proj, out_proj, final_norm]

    chains = [(qb, hk) for qb in range(TQ // WINDOW) for hk in range(N_KV_HEADS)]
    st_next = scores(*chains[0])
    pending = []
    for c, chain in enumerate(chains):
        st = st_next
        if c + 1 < len(chains):
            st_next = scores(*chains[c + 1])
        finish(*chain, st)
        if pending:
            pending.pop(0)()
        if chain[1] == N_KV_HEADS - 1:
            pending.extend(dense_pieces(chain[0]))
    for piece in pending:
        piece()


def _pad_pitch(w):
    return jnp.pad(w, ((0, 0), (0, PITCH_PAD)))


def _resident(shape):
    zeros = (0,) * len(shape)
    return pl.BlockSpec(shape, lambda b, s: zeros, pipeline_mode=pl.Buffered(1))


def _rope_tables(seq):
    inv_freq = ROPE_THETA ** (-jnp.arange(0, HEAD_DIM, 2, dtype=F32) / HEAD_DIM)
    ang = jnp.arange(seq, dtype=F32)[:, None] * inv_freq[None, :]
    cos, sin = jnp.cos(ang), jnp.sin(ang)
    scale = HEAD_DIM ** -0.5
    cosq = (scale * cos).T
    sinq = (scale * sin).T
    reps = LANES // HALF_DIM
    cosk = jnp.tile(cos, (1, reps))
    sign = jnp.where((jnp.arange(LANES) % HEAD_DIM) >= HALF_DIM, 1.0, -1.0).astype(F32)
    sink = jnp.tile(sin, (1, reps)) * sign[None, :]
    return cosq, sinq, cosk, sink


def _band_bias():
    kj = np.arange(2 * WINDOW)[:, None]
    qi = np.arange(WINDOW)[None, :]
    rel = qi + WINDOW - kj
    in_window = (rel >= 0) & (rel < WINDOW)
    later = np.where(in_window, 0.0, MASK_VALUE)
    first = np.where(in_window & (kj >= WINDOW), 0.0, MASK_VALUE)
    both = np.stack([first, later]).astype(np.float32)
    return jnp.asarray(np.tile(both, (1, 1, GROUP)))


def _half_scaled_columns():
    scale = np.ones((IN_WIDTH,), np.float32)
    scale[OFF_GLU_A:OFF_Q] = 0.5
    scale[OFF_ATTN_GATE:IN_WIDTH] = 0.5
    return jnp.asarray(scale)


def kernel(x, norm_g, w_in, conv_dw_w, conv_dw_b, conv_ln_g, conv_ln_b, w_conv_out,
           attn_sinks, w_attn_out, w_out, final_norm_g):
    batch, seq, d = x.shape
    assert d == D_MODEL and seq % TM == 0 and seq % TQ == 0
    assert norm_g.shape[0] == 1, "single-layer block"

    w_scaled = w_in[0] * _half_scaled_columns()[None, :]
    w = w_scaled.astype(BF16)
    wft = _pad_pitch(
        jnp.concatenate([w_scaled[:, OFF_Q:OFF_K], w_scaled[:, OFF_V:OFF_MERGE]], axis=1)
        .T.astype(BF16))
    cosq, sinq, cosk, sink = _rope_tables(seq)
    row = lambda v: v.reshape(1, -1).astype(F32)

    tok = lambda width: pl.BlockSpec((1, TM, width), lambda b, s: (b, s, 0))
    feat = lambda rows: pl.BlockSpec((1, rows, TM), lambda b, s: (b, 0, s))
    cg, qt, k, vt, sga, gates = pl.pallas_call(
        _proj_body,
        grid=(batch, seq // TM),
        in_specs=[
            tok(D_MODEL),
            _resident((1, D_MODEL)),
            _resident((D_MODEL, IN_WIDTH)),
            _resident((FEAT_ROWS, D_MODEL + PITCH_PAD)),
            _resident((CONV_KERNEL, D_MODEL)),
            _resident((1, D_MODEL)),
            _resident((1, D_MODEL)),
            _resident((1, D_MODEL)),
            pl.BlockSpec((HALF_DIM, TM), lambda b, s: (0, s)),
            pl.BlockSpec((HALF_DIM, TM), lambda b, s: (0, s)),
            pl.BlockSpec((TM, LANES), lambda b, s: (s, 0)),
            pl.BlockSpec((TM, LANES), lambda b, s: (s, 0)),
        ],
        out_specs=[tok(D_MODEL), feat(ATTN_WIDTH), tok(KV_WIDTH), feat(KV_WIDTH),
                   feat(ATTN_WIDTH), tok(2 * D_MODEL)],
        out_shape=[
            jax.ShapeDtypeStruct((batch, seq, D_MODEL), BF16),
            jax.ShapeDtypeStruct((batch, ATTN_WIDTH, seq), BF16),
            jax.ShapeDtypeStruct((batch, seq, KV_WIDTH), BF16),
            jax.ShapeDtypeStruct((batch, KV_WIDTH, seq), BF16),
            jax.ShapeDtypeStruct((batch, ATTN_WIDTH, seq), BF16),
            jax.ShapeDtypeStruct((batch, seq, 2 * D_MODEL), BF16),
        ],
        scratch_shapes=[
            pltpu.VMEM((SUBLANES, N_STRIPS, TM + CONV_HALO, LANES), F32),
            pltpu.VMEM((TM, D_MODEL), F32),
            pltpu.VMEM((TM, D_MODEL), BF16),
            pltpu.VMEM((D_MODEL, TM), BF16),
        ],
        compiler_params=pltpu.CompilerParams(
            dimension_semantics=("arbitrary", "arbitrary"),
            vmem_limit_bytes=VMEM_LIMIT),
        name="proj_call",
    )(x, row(norm_g[0]), w, wft, conv_dw_w[0].astype(F32), row(conv_dw_b[0]),
      row(0.5 * conv_ln_g[0]), row(0.5 * conv_ln_b[0]), cosq, sinq, cosk, sink)

    sinkrow = jnp.repeat(attn_sinks[0].astype(F32).reshape(N_KV_HEADS, 1, GROUP),
                         WINDOW, axis=2)
    wout = _pad_pitch(
        jnp.concatenate([w_conv_out[0], w_attn_out[0], 0.5 * w_out[0]], axis=1).astype(BF16))
    blocks_per_step = TQ // WINDOW
    prev = lambda s: jnp.maximum(s * blocks_per_step - 1, 0)
    tokq = lambda width: pl.BlockSpec((1, TQ, width), lambda b, s: (b, s, 0))
    featq = lambda rows: pl.BlockSpec((1, rows, TQ), lambda b, s: (b, 0, s))
    out = pl.pallas_call(
        _mix_body,
        grid=(batch, seq // TQ),
        in_specs=[
            featq(ATTN_WIDTH),
            tokq(KV_WIDTH),
            pl.BlockSpec((1, WINDOW, KV_WIDTH), lambda b, s: (b, prev(s), 0)),
            featq(KV_WIDTH),
            pl.BlockSpec((1, KV_WIDTH, WINDOW), lambda b, s: (b, 0, prev(s))),
            featq(ATTN_WIDTH),
            tokq(2 * D_MODEL),
            tokq(D_MODEL),
            tokq(D_MODEL),
            _resident((D_MODEL, 3 * D_MODEL + PITCH_PAD)),
            _resident((2, 2 * WINDOW, GROUP * WINDOW)),
            _resident((N_KV_HEADS, 1, GROUP * WINDOW)),
            _resident((1, D_MODEL)),
        ],
        out_specs=tokq(D_MODEL),
        out_shape=jax.ShapeDtypeStruct((batch, seq, D_MODEL), x.dtype),
        scratch_shapes=[pltpu.VMEM((ATTN_WIDTH, TQ), F32)],
        compiler_params=pltpu.CompilerParams(
            dimension_semantics=("arbitrary", "arbitrary"),
            vmem_limit_bytes=VMEM_LIMIT),
        name="mix_call",
    )(qt, k, k, vt, vt, sga, gates, cg, x, wout, _band_bias(), sinkrow, row(final_norm_g))
    return out
```

```python
import numpy as np
import jax
import jax.numpy as jnp
from jax import lax
from jax.experimental import pallas as pl
from jax.experimental.pallas import tpu as pltpu

F32 = jnp.float32
BF16 = jnp.bfloat16

D_MODEL = 1024
CONV_KERNEL = 31
HEAD_DIM = 64
HALF_DIM = HEAD_DIM // 2
N_Q_HEADS = 16
N_KV_HEADS = 4
GROUP = N_Q_HEADS // N_KV_HEADS
ATTN_WIDTH = N_Q_HEADS * HEAD_DIM
KV_WIDTH = N_KV_HEADS * HEAD_DIM
WINDOW = 128
ROPE_THETA = 10000.0
RMS_EPS = 1e-5
LN_EPS = 1e-5
MASK_VALUE = -1e30

OFF_GLU_A = 0
OFF_GLU_B = OFF_GLU_A + D_MODEL
OFF_CONV_GATE = OFF_GLU_B + D_MODEL
OFF_Q = OFF_CONV_GATE + D_MODEL
OFF_K = OFF_Q + ATTN_WIDTH
OFF_V = OFF_K + KV_WIDTH
OFF_ATTN_GATE = OFF_V + KV_WIDTH
OFF_MERGE = OFF_ATTN_GATE + ATTN_WIDTH
IN_WIDTH = OFF_MERGE + 2 * D_MODEL

ROW_Q = 0
ROW_V = ROW_Q + ATTN_WIDTH
ROW_GATE = ROW_V + KV_WIDTH
FEAT_ROWS = ROW_GATE + ATTN_WIDTH

SUBLANES = 8
LANES = 128
N_STRIPS = D_MODEL // LANES

TM = 512
TQ = 512
DENSE_BLOCKS = 1
CONV_HALO = 32
CONV_ROWS = 64
GLU_COLS = 256
PITCH_PAD = 128
VMEM_LIMIT = 60 * 1024 * 1024


def _dot(a, b):
    return jnp.dot(a, b, preferred_element_type=F32)


def _half_silu(half):
    return half * (1.0 + jnp.tanh(half))


def _proj_body(x_ref, ng_ref, w_ref, wft_ref, cw_ref, cb_ref, lg_ref, lb_ref,
               cosq_ref, sinq_ref, cosk_ref, sink_ref,
               cg_ref, qt_ref, k_ref, vt_ref, sga_ref, gates_ref,
               gbuf, cbuf, h_scr, ht_scr):
    s = pl.program_id(1)
    x = x_ref[0]
    ms = jnp.mean(x * x, axis=-1, keepdims=True)
    h_scr[...] = (x * lax.rsqrt(ms + RMS_EPS) * ng_ref[...]).astype(BF16)

    def transpose_h():
        ht_scr[...] = h_scr[...].T

    def proj(lo, hi):
        return _dot(h_scr[...], w_ref[:, lo:hi])

    def proj_t(lo, hi):
        return _dot(wft_ref[lo:hi, 0:D_MODEL], ht_scr[...])

    def glu_piece(g):
        def piece():
            a = OFF_GLU_A + g * GLU_COLS
            b = OFF_GLU_B + g * GLU_COLS
            glu = proj(a, a + GLU_COLS) * (1.0 + jnp.tanh(proj(b, b + GLU_COLS)))
            for i in range(GLU_COLS // LANES):
                c = g * (GLU_COLS // LANES) + i
                gbuf[0, c, CONV_HALO:CONV_HALO + TM, :] = glu[:, i * LANES:(i + 1) * LANES]
                for r in range(1, SUBLANES):
                    gbuf[r, c, 0:shifted_rows, :] = gbuf[0, c, r:r + shifted_rows, :]
        return piece

    @pl.when(s == 0)
    def _():
        gbuf[0, :, 0:CONV_HALO, :] = jnp.zeros((N_STRIPS, CONV_HALO, LANES), F32)

    shifted_rows = TM + CONV_HALO - SUBLANES
    first_row = CONV_HALO - (CONV_KERNEL - 1)

    def conv_strip(c):
        cols = slice(c * LANES, (c + 1) * LANES)
        taps = [cw_ref[j:j + 1, cols] for j in range(CONV_KERNEL)]
        bias = cb_ref[:, cols]
        for r0 in range(0, TM, CONV_ROWS):
            acc = bias
            for r in range(SUBLANES):
                js = [j for j in range(CONV_KERNEL) if (first_row + j) % SUBLANES == r]
                tops = [(first_row + j) // SUBLANES * SUBLANES + r0 for j in js]
                slab = gbuf[r, c, min(tops):max(tops) + CONV_ROWS, :]
                for j, top in zip(js, tops):
                    lo = top - min(tops)
                    acc = acc + slab[lo:lo + CONV_ROWS] * taps[j]
            cbuf[r0:r0 + CONV_ROWS, cols] = acc
        gbuf[0, c, 0:CONV_HALO, :] = gbuf[0, c, TM:TM + CONV_HALO, :]

    def conv_gate(lo, hi):
        cg_ref[0, :, lo:hi] = _half_silu(
            proj(OFF_CONV_GATE + lo, OFF_CONV_GATE + hi)).astype(BF16)

    def merge_gates(lo, hi):
        gates_ref[0, :, lo:hi] = (
            1.0 + jnp.tanh(proj(OFF_MERGE + lo, OFF_MERGE + hi))).astype(BF16)

    def keys():
        kf = proj(OFF_K, OFF_V)
        lane = lax.broadcasted_iota(jnp.int32, (TM, LANES), 1)
        upper_half = (lane % HEAD_DIM) >= HALF_DIM
        cosk = cosk_ref[...]
        sink = sink_ref[...]
        for c in range(KV_WIDTH // LANES):
            xs = kf[:, c * LANES:(c + 1) * LANES]
            rot = jnp.where(upper_half,
                            pltpu.roll(xs, HALF_DIM, 1),
                            pltpu.roll(xs, LANES - HALF_DIM, 1))
            k_ref[0, :, c * LANES:(c + 1) * LANES] = (xs * cosk + rot * sink).astype(BF16)

    def store_feat(ref, lo, val):
        for t in range(TM // WINDOW):
            ref[0, t, lo:lo + val.shape[0], :] = val[:, t * WINDOW:(t + 1) * WINDOW].astype(BF16)

    def values():
        store_feat(vt_ref, 0, proj_t(ROW_V, ROW_GATE))

    def queries(lo, hi):
        qt = proj_t(ROW_Q + lo, ROW_Q + hi)
        cosq = cosq_ref[...]
        sinq = sinq_ref[...]
        for a in range(0, hi - lo, HEAD_DIM):
            t1 = qt[a:a + HALF_DIM]
            t2 = qt[a + HALF_DIM:a + HEAD_DIM]
            store_feat(qt_ref, lo + a, t1 * cosq - t2 * sinq)
            store_feat(qt_ref, lo + a + HALF_DIM, t2 * cosq + t1 * sinq)

    def attn_gate(lo, hi):
        store_feat(sga_ref, lo, _half_silu(proj_t(ROW_GATE + lo, ROW_GATE + hi)))

    def layer_norm():
        for r0 in range(0, TM, CONV_ROWS):
            rows = slice(r0, r0 + CONV_ROWS)
            conv = cbuf[rows, :]
            mu = jnp.mean(conv, axis=-1, keepdims=True)
            cen = conv - mu
            var = jnp.mean(cen * cen, axis=-1, keepdims=True)
            half_ln = cen * lax.rsqrt(var + LN_EPS) * lg_ref[...] + lb_ref[...]
            gate = cg_ref[0, rows, :].astype(F32)
            cg_ref[0, rows, :] = (_half_silu(half_ln) * gate).astype(BF16)

    n_blocks = [0]

    def block(*steps):
        n_blocks[0] += 1

        @pl.when(s > -n_blocks[0])
        def _():
            for step in steps:
                step()

    half_q = ATTN_WIDTH // 2
    for g in range(D_MODEL // GLU_COLS):
        glu_piece(g)()
    block(lambda: conv_strip(0), lambda: conv_gate(0, 768),
          lambda: conv_strip(1), lambda: conv_gate(768, D_MODEL), lambda: merge_gates(0, 512),
          transpose_h)
    block(lambda: conv_strip(2), lambda: merge_gates(512, 1280),
          lambda: conv_strip(3), lambda: merge_gates(1280, 2 * D_MODEL))
    block(lambda: conv_strip(4), keys, lambda: queries(0, half_q),
          lambda: conv_strip(5), lambda: queries(half_q, ATTN_WIDTH), values)
    block(lambda: conv_strip(6), lambda: attn_gate(0, half_q),
          lambda: conv_strip(7), lambda: attn_gate(half_q, ATTN_WIDTH))
    block(layer_norm)


def _mix_body(qt_ref, k_ref, kp_ref, vt_ref, vtp_ref, sga_ref, gates_ref, cg_ref, x_ref,
              wout_ref, bias_ref, sinkrow_ref, fg_ref,
              o_ref, at_scr):
    s = pl.program_id(1)
    first_variant = jnp.where(s == 0, 0, 1)
    gq = GROUP * WINDOW

    def scores(qb, hk):
        if qb == 0:
            kband = jnp.concatenate([kp_ref[0], k_ref[0, 0:WINDOW, :]], axis=0)
            bias = bias_ref[first_variant]
        else:
            kband = k_ref[0, (qb - 1) * WINDOW:(qb + 1) * WINDOW, :]
            bias = bias_ref[1]
        rq = jnp.concatenate(
            [qt_ref[0, qb, (hk * GROUP + g) * HEAD_DIM:(hk * GROUP + g + 1) * HEAD_DIM, :]
             for g in range(GROUP)], axis=1)
        blocks = []
        if hk > 0:
            blocks.append(jnp.zeros((hk * HEAD_DIM, gq), BF16))
        blocks.append(rq)
        if hk < N_KV_HEADS - 1:
            blocks.append(jnp.zeros(((N_KV_HEADS - 1 - hk) * HEAD_DIM, gq), BF16))
        rhs = jnp.concatenate(blocks, axis=0)
        return _dot(kband, rhs) + bias

    def finish(qb, hk, st):
        q_lanes = slice(qb * WINDOW, (qb + 1) * WINDOW)
        v_prev = vtp_ref[0, 0] if qb == 0 else vt_ref[0, qb - 1]
        vband = jnp.concatenate([v_prev, vt_ref[0, qb]], axis=1)
        sink = sinkrow_ref[hk]
        m = jnp.maximum(jnp.max(st, axis=0, keepdims=True), sink)
        p = jnp.exp(st - m)
        den = jnp.sum(p, axis=0, keepdims=True) + jnp.exp(sink - m)
        ot = _dot(vband[hk * HEAD_DIM:(hk + 1) * HEAD_DIM, :], p.astype(BF16))
        ot = ot * (1.0 / den)
        for g in range(GROUP):
            lo = (hk * GROUP + g) * HEAD_DIM
            at_scr[lo:lo + HEAD_DIM, q_lanes] = ot[:, g * WINDOW:(g + 1) * WINDOW]

    def dense_pieces(qb0):
        rows = slice(qb0 * WINDOW, (qb0 + DENSE_BLOCKS) * WINDOW)
        vals = {}

        def conv_proj():
            vals["y_conv"] = _dot(cg_ref[0, rows, :], wout_ref[:, 0:D_MODEL])

        def attn_proj():
            gate = jnp.concatenate(
                [sga_ref[0, qb0 + i] for i in range(DENSE_BLOCKS)], axis=1).astype(F32)
            gated = (at_scr[:, rows] * gate).astype(BF16)
            vals["y_attn"] = lax.dot_general(
                gated, wout_ref[:, D_MODEL:2 * D_MODEL], (((0,), (0,)), ((), ())),
                preferred_element_type=F32)

        def out_proj():
            gates = gates_ref[0, rows, :].astype(F32)
            merged = (gates[:, :D_MODEL] * vals.pop("y_conv")
                      + gates[:, D_MODEL:] * vals.pop("y_attn"))
            vals["xo"] = x_ref[0, rows, :] + _dot(merged.astype(BF16),
                                                  wout_ref[:, 2 * D_MODEL:3 * D_MODEL])

        def final_norm():
            xo = vals.pop("xo")
            ms = jnp.mean(xo * xo, axis=-1, keepdims=True)
            o_ref[0, rows, :] = xo * lax.rsqrt(ms + RMS_EPS) * fg_ref[...]

        return [conv_proj, attn_proj, out_proj, final_norm]

    chains = [(qb, hk) for qb in range(TQ // WINDOW) for hk in range(N_KV_HEADS)]
    st_next = scores(*chains[0])
    pending = []
    for c, chain in enumerate(chains):
        st = st_next
        if c + 1 < len(chains):
            st_next = scores(*chains[c + 1])
        finish(*chain, st)
        if pending:
            pending.pop(0)()
        if chain[1] == N_KV_HEADS - 1 and (chain[0] + 1) % DENSE_BLOCKS == 0:
            pending.extend(dense_pieces(chain[0] + 1 - DENSE_BLOCKS))
    for piece in pending:
        piece()


def _pad_pitch(w):
    return jnp.pad(w, ((0, 0), (0, PITCH_PAD)))


def _resident(shape):
    zeros = (0,) * len(shape)
    return pl.BlockSpec(shape, lambda b, s: zeros, pipeline_mode=pl.Buffered(1))


def _rope_tables(seq):
    inv_freq = ROPE_THETA ** (-jnp.arange(0, HEAD_DIM, 2, dtype=F32) / HEAD_DIM)
    ang = jnp.arange(seq, dtype=F32)[:, None] * inv_freq[None, :]
    cos, sin = jnp.cos(ang), jnp.sin(ang)
    scale = HEAD_DIM ** -0.5
    cosq = (scale * cos).T
    sinq = (scale * sin).T
    reps = LANES // HALF_DIM
    cosk = jnp.tile(cos, (1, reps))
    sign = jnp.where((jnp.arange(LANES) % HEAD_DIM) >= HALF_DIM, 1.0, -1.0).astype(F32)
    sink = jnp.tile(sin, (1, reps)) * sign[None, :]
    return cosq, sinq, cosk, sink


def _band_bias():
    kj = np.arange(2 * WINDOW)[:, None]
    qi = np.arange(WINDOW)[None, :]
    rel = qi + WINDOW - kj
    in_window = (rel >= 0) & (rel < WINDOW)
    later = np.where(in_window, 0.0, MASK_VALUE)
    first = np.where(in_window & (kj >= WINDOW), 0.0, MASK_VALUE)
    both = np.stack([first, later]).astype(np.float32)
    return jnp.asarray(np.tile(both, (1, 1, GROUP)))


def _half_scaled_columns():
    scale = np.ones((IN_WIDTH,), np.float32)
    scale[OFF_GLU_A:OFF_Q] = 0.5
    scale[OFF_ATTN_GATE:IN_WIDTH] = 0.5
    return jnp.asarray(scale)


def kernel(x, norm_g, w_in, conv_dw_w, conv_dw_b, conv_ln_g, conv_ln_b, w_conv_out,
           attn_sinks, w_attn_out, w_out, final_norm_g):
    batch, seq, d = x.shape
    assert d == D_MODEL and seq % TM == 0 and seq % TQ == 0
    assert norm_g.shape[0] == 1, "single-layer block"

    w_scaled = w_in[0] * _half_scaled_columns()[None, :]
    w = w_scaled.astype(BF16)
    wft = _pad_pitch(
        jnp.concatenate([w_scaled[:, OFF_Q:OFF_K], w_scaled[:, OFF_V:OFF_MERGE]], axis=1)
        .T.astype(BF16))
    cosq, sinq, cosk, sink = _rope_tables(seq)
    row = lambda v: v.reshape(1, -1).astype(F32)

    tok = lambda width: pl.BlockSpec((1, TM, width), lambda b, s: (b, s, 0))
    feat = lambda rows: pl.BlockSpec((1, TM // WINDOW, rows, WINDOW), lambda b, s: (b, s, 0, 0))
    feat_shape = lambda rows: jax.ShapeDtypeStruct((batch, seq // WINDOW, rows, WINDOW), BF16)
    cg, qt, k, vt, sga, gates = pl.pallas_call(
        _proj_body,
        grid=(batch, seq // TM),
        in_specs=[
            tok(D_MODEL),
            _resident((1, D_MODEL)),
            _resident((D_MODEL, IN_WIDTH)),
            _resident((FEAT_ROWS, D_MODEL + PITCH_PAD)),
            _resident((CONV_KERNEL, D_MODEL)),
            _resident((1, D_MODEL)),
            _resident((1, D_MODEL)),
            _resident((1, D_MODEL)),
            pl.BlockSpec((HALF_DIM, TM), lambda b, s: (0, s)),
            pl.BlockSpec((HALF_DIM, TM), lambda b, s: (0, s)),
            pl.BlockSpec((TM, LANES), lambda b, s: (s, 0)),
            pl.BlockSpec((TM, LANES), lambda b, s: (s, 0)),
        ],
        out_specs=[tok(D_MODEL), feat(ATTN_WIDTH), tok(KV_WIDTH), feat(KV_WIDTH),
                   feat(ATTN_WIDTH), tok(2 * D_MODEL)],
        out_shape=[
            jax.ShapeDtypeStruct((batch, seq, D_MODEL), BF16),
            feat_shape(ATTN_WIDTH),
            jax.ShapeDtypeStruct((batch, seq, KV_WIDTH), BF16),
            feat_shape(KV_WIDTH),
            feat_shape(ATTN_WIDTH),
            jax.ShapeDtypeStruct((batch, seq, 2 * D_MODEL), BF16),
        ],
        scratch_shapes=[
            pltpu.VMEM((SUBLANES, N_STRIPS, TM + CONV_HALO, LANES), F32),
            pltpu.VMEM((TM, D_MODEL), F32),
            pltpu.VMEM((TM, D_MODEL), BF16),
            pltpu.VMEM((D_MODEL, TM), BF16),
        ],
        compiler_params=pltpu.CompilerParams(
            dimension_semantics=("arbitrary", "arbitrary"),
            vmem_limit_bytes=VMEM_LIMIT),
        name="proj_call",
    )(x, row(norm_g[0]), w, wft, conv_dw_w[0].astype(F32), row(conv_dw_b[0]),
      row(0.5 * conv_ln_g[0]), row(0.5 * conv_ln_b[0]), cosq, sinq, cosk, sink)

    sinkrow = jnp.repeat(attn_sinks[0].astype(F32).reshape(N_KV_HEADS, 1, GROUP),
                         WINDOW, axis=2)
    wout = _pad_pitch(
        jnp.concatenate([w_conv_out[0], w_attn_out[0], 0.5 * w_out[0]], axis=1).astype(BF16))
    blocks_per_step = TQ // WINDOW
    prev = lambda s: jnp.maximum(s * blocks_per_step - 1, 0)
    tokq = lambda width: pl.BlockSpec((1, TQ, width), lambda b, s: (b, s, 0))
    featq = lambda rows: pl.BlockSpec((1, blocks_per_step, rows, WINDOW),
                                      lambda b, s: (b, s, 0, 0))
    out = pl.pallas_call(
        _mix_body,
        grid=(batch, seq // TQ),
        in_specs=[
            featq(ATTN_WIDTH),
            tokq(KV_WIDTH),
            pl.BlockSpec((1, WINDOW, KV_WIDTH), lambda b, s: (b, prev(s), 0)),
            featq(KV_WIDTH),
            pl.BlockSpec((1, 1, KV_WIDTH, WINDOW), lambda b, s: (b, prev(s), 0, 0)),
            featq(ATTN_WIDTH),
            tokq(2 * D_MODEL),
            tokq(D_MODEL),
            tokq(D_MODEL),
            _resident((D_MODEL, 3 * D_MODEL + PITCH_PAD)),
            _resident((2, 2 * WINDOW, GROUP * WINDOW)),
            _resident((N_KV_HEADS, 1, GROUP * WINDOW)),
            _resident((1, D_MODEL)),
        ],
        out_specs=tokq(D_MODEL),
        out_shape=jax.ShapeDtypeStruct((batch, seq, D_MODEL), x.dtype),
        scratch_shapes=[pltpu.VMEM((ATTN_WIDTH, TQ), F32)],
        compiler_params=pltpu.CompilerParams(
            dimension_semantics=("arbitrary", "arbitrary"),
            vmem_limit_bytes=VMEM_LIMIT),
        name="mix_call",
    )(qt, k, k, vt, vt, sga, gates, cg, x, wout, _band_bias(), sinkrow, row(final_norm_g))
    return out
```

```python
import numpy as np
import jax
import jax.numpy as jnp
from jax import lax
from jax.experimental import pallas as pl
from jax.experimental.pallas import tpu as pltpu

F32 = jnp.float32
BF16 = jnp.bfloat16

D_MODEL = 1024
CONV_KERNEL = 31
HEAD_DIM = 64
HALF_DIM = HEAD_DIM // 2
N_Q_HEADS = 16
N_KV_HEADS = 4
GROUP = N_Q_HEADS // N_KV_HEADS
ATTN_WIDTH = N_Q_HEADS * HEAD_DIM
KV_WIDTH = N_KV_HEADS * HEAD_DIM
WINDOW = 128
ROPE_THETA = 10000.0
RMS_EPS = 1e-5
LN_EPS = 1e-5
MASK_VALUE = -1e30
LOG2_E = 1.4426950408889634

OFF_GLU_A = 0
OFF_GLU_B = OFF_GLU_A + D_MODEL
OFF_CONV_GATE = OFF_GLU_B + D_MODEL
OFF_Q = OFF_CONV_GATE + D_MODEL
OFF_K = OFF_Q + ATTN_WIDTH
OFF_V = OFF_K + KV_WIDTH
OFF_ATTN_GATE = OFF_V + KV_WIDTH
OFF_MERGE = OFF_ATTN_GATE + ATTN_WIDTH
IN_WIDTH = OFF_MERGE + 2 * D_MODEL

ROW_Q = 0
ROW_V = ROW_Q + ATTN_WIDTH
ROW_GATE = ROW_V + KV_WIDTH
FEAT_ROWS = ROW_GATE + ATTN_WIDTH

SUBLANES = 8
LANES = 128
N_STRIPS = D_MODEL // LANES

TM = 512
TQ = 512
DENSE_BLOCKS = 1
CONV_HALO = 32
CONV_ROWS = 64
GLU_COLS = 256
PITCH_PAD = 128
VMEM_LIMIT = 60 * 1024 * 1024


def _dot(a, b):
    return jnp.dot(a, b, preferred_element_type=F32)


def _half_silu(half):
    return half * (1.0 + jnp.tanh(half))


def _proj_body(x_ref, ng_ref, w_ref, wft_ref, cw_ref, cb_ref, lg_ref, lb_ref,
               cosq_ref, sinq_ref, cosk_ref, sink_ref,
               cg_ref, qt_ref, k_ref, vt_ref, sga_ref, gates_ref,
               gbuf, cbuf, h_scr, ht_scr):
    s = pl.program_id(1)
    x = x_ref[0]
    ms = jnp.mean(x * x, axis=-1, keepdims=True)
    h_scr[...] = (x * lax.rsqrt(ms + RMS_EPS) * ng_ref[...]).astype(BF16)

    def transpose_h():
        ht_scr[...] = h_scr[...].T

    def proj(lo, hi):
        return _dot(h_scr[...], w_ref[:, lo:hi])

    def proj_t(lo, hi):
        return _dot(wft_ref[lo:hi, 0:D_MODEL], ht_scr[...])

    def glu_piece(g):
        def piece():
            a = OFF_GLU_A + g * GLU_COLS
            b = OFF_GLU_B + g * GLU_COLS
            glu = proj(a, a + GLU_COLS) * (1.0 + jnp.tanh(proj(b, b + GLU_COLS)))
            for i in range(GLU_COLS // LANES):
                c = g * (GLU_COLS // LANES) + i
                gbuf[0, c, CONV_HALO:CONV_HALO + TM, :] = glu[:, i * LANES:(i + 1) * LANES]
                for r in range(1, SUBLANES):
                    gbuf[r, c, 0:shifted_rows, :] = gbuf[0, c, r:r + shifted_rows, :]
        return piece

    @pl.when(s == 0)
    def _():
        gbuf[0, :, 0:CONV_HALO, :] = jnp.zeros((N_STRIPS, CONV_HALO, LANES), F32)

    shifted_rows = TM + CONV_HALO - SUBLANES
    first_row = CONV_HALO - (CONV_KERNEL - 1)

    def conv_strip(c):
        cols = slice(c * LANES, (c + 1) * LANES)
        taps = [cw_ref[j:j + 1, cols] for j in range(CONV_KERNEL)]
        bias = cb_ref[:, cols]
        for r0 in range(0, TM, CONV_ROWS):
            acc = bias
            for r in range(SUBLANES):
                js = [j for j in range(CONV_KERNEL) if (first_row + j) % SUBLANES == r]
                tops = [(first_row + j) // SUBLANES * SUBLANES + r0 for j in js]
                slab = gbuf[r, c, min(tops):max(tops) + CONV_ROWS, :]
                for j, top in zip(js, tops):
                    lo = top - min(tops)
                    acc = acc + slab[lo:lo + CONV_ROWS] * taps[j]
            cbuf[r0:r0 + CONV_ROWS, cols] = acc
        gbuf[0, c, 0:CONV_HALO, :] = gbuf[0, c, TM:TM + CONV_HALO, :]

    def conv_gate(lo, hi):
        cg_ref[0, :, lo:hi] = _half_silu(
            proj(OFF_CONV_GATE + lo, OFF_CONV_GATE + hi)).astype(BF16)

    def merge_gates(lo, hi):
        gates_ref[0, :, lo:hi] = (
            1.0 + jnp.tanh(proj(OFF_MERGE + lo, OFF_MERGE + hi))).astype(BF16)

    def keys():
        kf = proj(OFF_K, OFF_V)
        lane = lax.broadcasted_iota(jnp.int32, (TM, LANES), 1)
        upper_half = (lane % HEAD_DIM) >= HALF_DIM
        cosk = cosk_ref[...]
        sink = sink_ref[...]
        for c in range(KV_WIDTH // LANES):
            xs = kf[:, c * LANES:(c + 1) * LANES]
            rot = jnp.where(upper_half,
                            pltpu.roll(xs, HALF_DIM, 1),
                            pltpu.roll(xs, LANES - HALF_DIM, 1))
            k_ref[0, :, c * LANES:(c + 1) * LANES] = (xs * cosk + rot * sink).astype(BF16)

    def store_feat(ref, lo, val):
        for t in range(TM // WINDOW):
            ref[0, t, lo:lo + val.shape[0], :] = val[:, t * WINDOW:(t + 1) * WINDOW].astype(BF16)

    def values():
        store_feat(vt_ref, 0, proj_t(ROW_V, ROW_GATE))

    def queries(lo, hi):
        qt = proj_t(ROW_Q + lo, ROW_Q + hi)
        cosq = cosq_ref[...]
        sinq = sinq_ref[...]
        for a in range(0, hi - lo, HEAD_DIM):
            t1 = qt[a:a + HALF_DIM]
            t2 = qt[a + HALF_DIM:a + HEAD_DIM]
            store_feat(qt_ref, lo + a, t1 * cosq - t2 * sinq)
            store_feat(qt_ref, lo + a + HALF_DIM, t2 * cosq + t1 * sinq)

    def attn_gate(lo, hi):
        store_feat(sga_ref, lo, _half_silu(proj_t(ROW_GATE + lo, ROW_GATE + hi)))

    def layer_norm():
        for r0 in range(0, TM, CONV_ROWS):
            rows = slice(r0, r0 + CONV_ROWS)
            conv = cbuf[rows, :]
            mu = jnp.mean(conv, axis=-1, keepdims=True)
            cen = conv - mu
            var = jnp.mean(cen * cen, axis=-1, keepdims=True)
            half_ln = cen * lax.rsqrt(var + LN_EPS) * lg_ref[...] + lb_ref[...]
            gate = cg_ref[0, rows, :].astype(F32)
            cg_ref[0, rows, :] = (_half_silu(half_ln) * gate).astype(BF16)

    n_blocks = [0]

    def block(*steps):
        n_blocks[0] += 1

        @pl.when(s > -n_blocks[0])
        def _():
            for step in steps:
                step()

    half_q = ATTN_WIDTH // 2
    for g in range(D_MODEL // GLU_COLS):
        glu_piece(g)()
    block(lambda: conv_strip(0), lambda: conv_gate(0, 768),
          lambda: conv_strip(1), lambda: conv_gate(768, D_MODEL), lambda: merge_gates(0, 512),
          transpose_h)
    block(lambda: conv_strip(2), lambda: merge_gates(512, 1280),
          lambda: conv_strip(3), lambda: merge_gates(1280, 2 * D_MODEL))
    block(lambda: conv_strip(4), keys, lambda: queries(0, half_q),
          lambda: conv_strip(5), lambda: queries(half_q, ATTN_WIDTH), values)
    block(lambda: conv_strip(6), lambda: attn_gate(0, half_q),
          lambda: conv_strip(7), lambda: attn_gate(half_q, ATTN_WIDTH))
    block(layer_norm)


def _mix_body(qt_ref, k_ref, kp_ref, vt_ref, vtp_ref, sga_ref, gates_ref, cg_ref, x_ref,
              wout_ref, bias_ref, sinkrow_ref, fg_ref,
              o_ref, at_scr):
    s = pl.program_id(1)
    first_variant = jnp.where(s == 0, 0, 1)
    gq = GROUP * WINDOW

    def scores(qb, hk):
        if qb == 0:
            kband = jnp.concatenate([kp_ref[0], k_ref[0, 0:WINDOW, :]], axis=0)
            bias = bias_ref[first_variant]
        else:
            kband = k_ref[0, (qb - 1) * WINDOW:(qb + 1) * WINDOW, :]
            bias = bias_ref[1]
        rq = jnp.concatenate(
            [qt_ref[0, qb, (hk * GROUP + g) * HEAD_DIM:(hk * GROUP + g + 1) * HEAD_DIM, :]
             for g in range(GROUP)], axis=1)
        blocks = []
        if hk > 0:
            blocks.append(jnp.zeros((hk * HEAD_DIM, gq), BF16))
        blocks.append(rq)
        if hk < N_KV_HEADS - 1:
            blocks.append(jnp.zeros(((N_KV_HEADS - 1 - hk) * HEAD_DIM, gq), BF16))
        rhs = jnp.concatenate(blocks, axis=0)
        return _dot(kband, rhs) + bias

    def finish(qb, hk, st):
        q_lanes = slice(qb * WINDOW, (qb + 1) * WINDOW)
        v_prev = vtp_ref[0, 0] if qb == 0 else vt_ref[0, qb - 1]
        vband = jnp.concatenate([v_prev, vt_ref[0, qb]], axis=1)
        sink = sinkrow_ref[hk]
        m = jnp.maximum(jnp.max(st, axis=0, keepdims=True), sink)
        p = jnp.exp2(st - m)
        den = jnp.sum(p, axis=0, keepdims=True) + jnp.exp2(sink - m)
        ot = _dot(vband[hk * HEAD_DIM:(hk + 1) * HEAD_DIM, :], p.astype(BF16))
        ot = ot * (1.0 / den)
        for g in range(GROUP):
            lo = (hk * GROUP + g) * HEAD_DIM
            at_scr[lo:lo + HEAD_DIM, q_lanes] = ot[:, g * WINDOW:(g + 1) * WINDOW]

    def dense_pieces(qb0):
        rows = slice(qb0 * WINDOW, (qb0 + DENSE_BLOCKS) * WINDOW)
        vals = {}

        def conv_proj():
            vals["y_conv"] = _dot(cg_ref[0, rows, :], wout_ref[:, 0:D_MODEL])

        def attn_proj():
            gate = jnp.concatenate(
                [sga_ref[0, qb0 + i] for i in range(DENSE_BLOCKS)], axis=1).astype(F32)
            gated = (at_scr[:, rows] * gate).astype(BF16)
            vals["y_attn"] = lax.dot_general(
                gated, wout_ref[:, D_MODEL:2 * D_MODEL], (((0,), (0,)), ((), ())),
                preferred_element_type=F32)

        def out_proj():
            gates = gates_ref[0, rows, :].astype(F32)
            merged = (gates[:, :D_MODEL] * vals.pop("y_conv")
                      + gates[:, D_MODEL:] * vals.pop("y_attn"))
            vals["xo"] = x_ref[0, rows, :] + _dot(merged.astype(BF16),
                                                  wout_ref[:, 2 * D_MODEL:3 * D_MODEL])

        def final_norm():
            xo = vals.pop("xo")
            ms = jnp.mean(xo * xo, axis=-1, keepdims=True)
            o_ref[0, rows, :] = xo * lax.rsqrt(ms + RMS_EPS) * fg_ref[...]

        return [conv_proj, attn_proj, out_proj, final_norm]

    chains = [(qb, hk) for qb in range(TQ // WINDOW) for hk in range(N_KV_HEADS)]
    st_next = scores(*chains[0])
    pending = []
    for c, chain in enumerate(chains):
        st = st_next
        if c + 1 < len(chains):
            st_next = scores(*chains[c + 1])
        finish(*chain, st)
        if pending:
            pending.pop(0)()
        if chain[1] == N_KV_HEADS - 1 and (chain[0] + 1) % DENSE_BLOCKS == 0:
            pending.extend(dense_pieces(chain[0] + 1 - DENSE_BLOCKS))
    for piece in pending:
        piece()


def _pad_pitch(w):
    return jnp.pad(w, ((0, 0), (0, PITCH_PAD)))


FEAT_TILE = 256


def _transpose_body(w_ref, o_ref):
    o_ref[:, 0:D_MODEL] = w_ref[...].T
    o_ref[:, D_MODEL:] = jnp.zeros((FEAT_TILE, PITCH_PAD), BF16)


def _feature_major_weights(w):
    def src_block(i):
        return jnp.where(i < ATTN_WIDTH // FEAT_TILE, OFF_Q // FEAT_TILE + i,
                         (OFF_V - ATTN_WIDTH) // FEAT_TILE + i)
    return pl.pallas_call(
        _transpose_body,
        grid=(FEAT_ROWS // FEAT_TILE,),
        in_specs=[pl.BlockSpec((D_MODEL, FEAT_TILE), lambda i: (0, src_block(i)))],
        out_specs=pl.BlockSpec((FEAT_TILE, D_MODEL + PITCH_PAD), lambda i: (i, 0)),
        out_shape=jax.ShapeDtypeStruct((FEAT_ROWS, D_MODEL + PITCH_PAD), BF16),
        name="weight_transpose_call",
    )(w)


def _resident(shape):
    zeros = (0,) * len(shape)
    return pl.BlockSpec(shape, lambda b, s: zeros, pipeline_mode=pl.Buffered(1))


def _rope_tables(seq):
    inv_freq = ROPE_THETA ** (-jnp.arange(0, HEAD_DIM, 2, dtype=F32) / HEAD_DIM)
    ang = jnp.arange(seq, dtype=F32)[:, None] * inv_freq[None, :]
    cos, sin = jnp.cos(ang), jnp.sin(ang)
    scale = HEAD_DIM ** -0.5 * LOG2_E
    cosq = (scale * cos).T
    sinq = (scale * sin).T
    reps = LANES // HALF_DIM
    cosk = jnp.tile(cos, (1, reps))
    sign = jnp.where((jnp.arange(LANES) % HEAD_DIM) >= HALF_DIM, 1.0, -1.0).astype(F32)
    sink = jnp.tile(sin, (1, reps)) * sign[None, :]
    return cosq, sinq, cosk, sink


def _band_bias():
    kj = np.arange(2 * WINDOW)[:, None]
    qi = np.arange(WINDOW)[None, :]
    rel = qi + WINDOW - kj
    in_window = (rel >= 0) & (rel < WINDOW)
    later = np.where(in_window, 0.0, MASK_VALUE)
    first = np.where(in_window & (kj >= WINDOW), 0.0, MASK_VALUE)
    both = np.stack([first, later]).astype(np.float32)
    return jnp.asarray(np.tile(both, (1, 1, GROUP)))


def _half_scaled_columns():
    scale = np.ones((IN_WIDTH,), np.float32)
    scale[OFF_GLU_A:OFF_Q] = 0.5
    scale[OFF_ATTN_GATE:IN_WIDTH] = 0.5
    return jnp.asarray(scale)


def kernel(x, norm_g, w_in, conv_dw_w, conv_dw_b, conv_ln_g, conv_ln_b, w_conv_out,
           attn_sinks, w_attn_out, w_out, final_norm_g):
    batch, seq, d = x.shape
    assert d == D_MODEL and seq % TM == 0 and seq % TQ == 0
    assert norm_g.shape[0] == 1, "single-layer block"

    w = (w_in[0] * _half_scaled_columns()[None, :]).astype(BF16)
    wft = _feature_major_weights(w)
    cosq, sinq, cosk, sink = _rope_tables(seq)
    row = lambda v: v.reshape(1, -1).astype(F32)

    tok = lambda width: pl.BlockSpec((1, TM, width), lambda b, s: (b, s, 0))
    feat = lambda rows: pl.BlockSpec((1, TM // WINDOW, rows, WINDOW), lambda b, s: (b, s, 0, 0))
    feat_shape = lambda rows: jax.ShapeDtypeStruct((batch, seq // WINDOW, rows, WINDOW), BF16)
    cg, qt, k, vt, sga, gates = pl.pallas_call(
        _proj_body,
        grid=(batch, seq // TM),
        in_specs=[
            tok(D_MODEL),
            _resident((1, D_MODEL)),
            _resident((D_MODEL, IN_WIDTH)),
            _resident((FEAT_ROWS, D_MODEL + PITCH_PAD)),
            _resident((CONV_KERNEL, D_MODEL)),
            _resident((1, D_MODEL)),
            _resident((1, D_MODEL)),
            _resident((1, D_MODEL)),
            pl.BlockSpec((HALF_DIM, TM), lambda b, s: (0, s)),
            pl.BlockSpec((HALF_DIM, TM), lambda b, s: (0, s)),
            pl.BlockSpec((TM, LANES), lambda b, s: (s, 0)),
            pl.BlockSpec((TM, LANES), lambda b, s: (s, 0)),
        ],
        out_specs=[tok(D_MODEL), feat(ATTN_WIDTH), tok(KV_WIDTH), feat(KV_WIDTH),
                   feat(ATTN_WIDTH), tok(2 * D_MODEL)],
        out_shape=[
            jax.ShapeDtypeStruct((batch, seq, D_MODEL), BF16),
            feat_shape(ATTN_WIDTH),
            jax.ShapeDtypeStruct((batch, seq, KV_WIDTH), BF16),
            feat_shape(KV_WIDTH),
            feat_shape(ATTN_WIDTH),
            jax.ShapeDtypeStruct((batch, seq, 2 * D_MODEL), BF16),
        ],
        scratch_shapes=[
            pltpu.VMEM((SUBLANES, N_STRIPS, TM + CONV_HALO, LANES), F32),
            pltpu.VMEM((TM, D_MODEL), F32),
            pltpu.VMEM((TM, D_MODEL), BF16),
            pltpu.VMEM((D_MODEL, TM), BF16),
        ],
        compiler_params=pltpu.CompilerParams(
            dimension_semantics=("arbitrary", "arbitrary"),
            vmem_limit_bytes=VMEM_LIMIT),
        name="proj_call",
    )(x, row(norm_g[0]), w, wft, conv_dw_w[0].astype(F32), row(conv_dw_b[0]),
      row(0.5 * conv_ln_g[0]), row(0.5 * conv_ln_b[0]), cosq, sinq, cosk, sink)

    sinkrow = jnp.repeat((LOG2_E * attn_sinks[0].astype(F32)).reshape(N_KV_HEADS, 1, GROUP),
                         WINDOW, axis=2)
    wout = _pad_pitch(
        jnp.concatenate([w_conv_out[0], w_attn_out[0], 0.5 * w_out[0]], axis=1).astype(BF16))
    blocks_per_step = TQ // WINDOW
    prev = lambda s: jnp.maximum(s * blocks_per_step - 1, 0)
    tokq = lambda width: pl.BlockSpec((1, TQ, width), lambda b, s: (b, s, 0))
    featq = lambda rows: pl.BlockSpec((1, blocks_per_step, rows, WINDOW),
                                      lambda b, s: (b, s, 0, 0))
    out = pl.pallas_call(
        _mix_body,
        grid=(batch, seq // TQ),
        in_specs=[
            featq(ATTN_WIDTH),
            tokq(KV_WIDTH),
            pl.BlockSpec((1, WINDOW, KV_WIDTH), lambda b, s: (b, prev(s), 0)),
            featq(KV_WIDTH),
            pl.BlockSpec((1, 1, KV_WIDTH, WINDOW), lambda b, s: (b, prev(s), 0, 0)),
            featq(ATTN_WIDTH),
            tokq(2 * D_MODEL),
            tokq(D_MODEL),
            tokq(D_MODEL),
            _resident((D_MODEL, 3 * D_MODEL + PITCH_PAD)),
            _resident((2, 2 * WINDOW, GROUP * WINDOW)),
            _resident((N_KV_HEADS, 1, GROUP * WINDOW)),
            _resident((1, D_MODEL)),
        ],
        out_specs=tokq(D_MODEL),
        out_shape=jax.ShapeDtypeStruct((batch, seq, D_MODEL), x.dtype),
        scratch_shapes=[pltpu.VMEM((ATTN_WIDTH, TQ), F32)],
        compiler_params=pltpu.CompilerParams(
            dimension_semantics=("arbitrary", "arbitrary"),
            vmem_limit_bytes=VMEM_LIMIT),
        name="mix_call",
    )(qt, k, k, vt, vt, sga, gates, cg, x, wout, _band_bias(), sinkrow, row(final_norm_g))
    return out
```
